```python
import math
import functools
import numpy as np
import jax
import jax.numpy as jnp
from jax import lax

D_MODEL = 2048
BATCH = 2
SEQ = 4096
DEPTH = 2

GRID_W = 64
CTX_LEN = 256
EPS = 1e-6

D_MIX = D_MODEL
N_MIXERS = 4
GROUP_W = D_MIX // N_MIXERS
HEAD_DIM = 128

ATT_HEADS = GROUP_W // HEAD_DIM
ATT_KV_HEADS = 2
ROPE_THETA = 10000.0
Q_BLOCK = 128

NA_HEADS = GROUP_W // HEAD_DIM
NA_ROWS = 8
NA_COLS = 16

SGU_CHUNK = 128
SGU_GROUPS = 4
SGU_GROUP_W = GROUP_W // SGU_GROUPS

SSM_HEAD_DIM = 64
SSM_HEADS = GROUP_W // SSM_HEAD_DIM
SSM_GROUPS = 2
SSM_HPG = SSM_HEADS // SSM_GROUPS
SSM_STATE = 128
SSM_CHUNK = 128
CONV_K = 5
CONV_CH = GROUP_W + 2 * SSM_GROUPS * SSM_STATE

D_FF = 5632
N_EXPERTS = 8
TOP_K = 2
D_FF_EXPERT = 7168
MOE_BLOCK = 128
N_DENSE = (DEPTH + 1) // 2
N_MOE = DEPTH // 2

Q_SIDE = (ATT_HEADS * HEAD_DIM, NA_HEADS * HEAD_DIM, GROUP_W, GROUP_W, GROUP_W, SSM_GROUPS * SSM_STATE)
K_SIDE = (ATT_KV_HEADS * HEAD_DIM, ATT_KV_HEADS * HEAD_DIM, NA_HEADS * HEAD_DIM, NA_HEADS * HEAD_DIM,
          GROUP_W, SSM_GROUPS * SSM_STATE, 2 * SSM_HEADS)
Q_COLS = sum(Q_SIDE)
IN_COLS = Q_COLS + sum(K_SIDE)

kernel_name = 'hymba_style_diffusion_hybrid_trunk'


def rms_norm(x, g):
    xf = x.astype(jnp.float32)
    y = xf * lax.rsqrt(jnp.mean(xf * xf, axis=-1, keepdims=True) + EPS)
    return (y * g.astype(jnp.float32)).astype(x.dtype)


def split_cols(t, widths):
    cuts = [int(v) for v in np.cumsum(widths)[:-1]]
    return jnp.split(t, cuts, axis=-1)


def heads(t, n):
    return t.reshape(t.shape[:-1] + (n, t.shape[-1] // n))


def modulation(cond, w_mod, b_mod, n):
    m = jax.nn.silu(cond) @ w_mod[:, : n * D_MODEL] + b_mod[: n * D_MODEL]
    return jnp.split(m[..., None, :], n, axis=-1)


def axial_rope(n_tokens):
    t = jnp.arange(n_tokens)
    row = (t // GRID_W).astype(jnp.float32)
    col = (t % GRID_W).astype(jnp.float32)
    n_freq = HEAD_DIM // 4
    inv = jnp.power(ROPE_THETA, -jnp.arange(n_freq, dtype=jnp.float32) / n_freq)
    ang = jnp.concatenate([row[:, None] * inv, col[:, None] * inv], axis=-1)
    return jnp.cos(ang), jnp.sin(ang)


def apply_rope(x, cos, sin):
    xf = x.astype(jnp.float32).reshape(x.shape[:-1] + (HEAD_DIM // 2, 2))
    x0, x1 = xf[..., 0], xf[..., 1]
    cs, sn = cos[None, :, None, :], sin[None, :, None, :]
    out = jnp.stack([x0 * cs - x1 * sn, x0 * sn + x1 * cs], axis=-1)
    return out.reshape(x.shape).astype(x.dtype)


def gqa_block_attention(q, k, v):
    bsz, lq, g, r, dh = q.shape
    nb = lq // Q_BLOCK
    qb = jnp.moveaxis(q.reshape(bsz, nb, Q_BLOCK, g, r, dh), 1, 0)
    scale = dh ** -0.5

    def one_block(qi):
        s = jnp.einsum('bqgrd,bkgd->bgrqk', qi, k).astype(jnp.float32) * scale
        p = jax.nn.softmax(s, axis=-1).astype(v.dtype)
        return jnp.einsum('bgrqk,bkgd->bqgrd', p, v)

    o = lax.map(one_block, qb)
    return jnp.moveaxis(o, 0, 1).reshape(bsz, lq, g * r * dh)


def neighbourhood_attention(q, k, v, k_ctx, v_ctx, rpb):
    bsz, s_len, n_h, dh = q.shape
    rows = s_len // GRID_W
    kh = min(NA_ROWS, rows)
    n_cb = GRID_W // NA_COLS
    band = 2 * NA_COLS
    n_loc = kh * band
    scale = dh ** -0.5
    qcol = jnp.arange(GRID_W).reshape(n_cb, NA_COLS)
    win_c0 = jnp.clip(qcol - NA_COLS // 2, 0, GRID_W - NA_COLS)
    band_c0 = jnp.clip(jnp.arange(n_cb) * NA_COLS - NA_COLS // 2, 0, GRID_W - band)
    kcol = band_c0[:, None] + jnp.arange(band)
    col_ok = (kcol[:, None, :] >= win_c0[:, :, None]) & (kcol[:, None, :] < win_c0[:, :, None] + NA_COLS)
    mask = jnp.broadcast_to(col_ok[:, :, None, :], (n_cb, NA_COLS, kh, band)).reshape(n_cb, NA_COLS, n_loc)
    dcol = jnp.clip(kcol[:, None, :] - qcol[:, :, None], 1 - NA_COLS, NA_COLS - 1) + NA_COLS - 1
    rpb32 = rpb.astype(jnp.float32)
    q_rows = jnp.moveaxis(q.reshape(bsz, rows, GRID_W, n_h, dh), 1, 0)

    def one_row(args):
        r, q_r = args
        krow = jnp.clip(r - kh // 2, 0, rows - kh) + jnp.arange(kh)
        idx = (krow[None, :, None] * GRID_W + kcol[:, None, :]).reshape(n_cb, n_loc)
        kb = jnp.take(k, idx, axis=1)
        vb = jnp.take(v, idx, axis=1)
        drow = krow - r + NA_ROWS - 1
        bias = rpb32[:, drow[None, None, :, None], dcol[:, :, None, :]].reshape(n_h, n_cb, NA_COLS, n_loc)
        qb = q_r.reshape(bsz, n_cb, NA_COLS, n_h, dh)
        s_loc = jnp.einsum('bnqhd,bnkhd->bhnqk', qb, kb).astype(jnp.float32) * scale + bias
        s_loc = jnp.where(mask, s_loc, -1e30)
        s_ctx = jnp.einsum('bnqhd,bchd->bhnqc', qb, k_ctx).astype(jnp.float32) * scale
        p = jax.nn.softmax(jnp.concatenate([s_loc, s_ctx], axis=-1), axis=-1).astype(v.dtype)
        o = (jnp.einsum('bhnqk,bnkhd->bnqhd', p[..., :n_loc], vb)
             + jnp.einsum('bhnqc,bchd->bnqhd', p[..., n_loc:], v_ctx))
        return o.reshape(bsz, GRID_W, n_h, dh)

    o = lax.map(one_row, (jnp.arange(rows), q_rows))
    return jnp.moveaxis(o, 0, 1).reshape(bsz, s_len, n_h * dh)


def spatial_gating_unit(u, v, norm_g, w_s, b_s):
    bsz, length, _ = v.shape
    n = length // SGU_CHUNK
    vn = rms_norm(v, norm_g).reshape(bsz, n, SGU_CHUNK, SGU_GROUPS, SGU_GROUP_W)
    mixed = jnp.einsum('gij,bnjgc->bnigc', w_s, vn) + b_s.T[:, :, None]
    return u * mixed.reshape(bsz, length, GROUP_W)


def depthwise_conv(t, w, b):
    y = lax.conv_general_dilated(t, w[:, None, :].astype(t.dtype), window_strides=(1,),
                                 padding=[(CONV_K // 2, CONV_K // 2)],
                                 dimension_numbers=('NWC', 'WIO', 'NWC'),
                                 feature_group_count=t.shape[-1])
    return y + b


def ssm_inputs(dx, db, dc, ddt, conv_w, conv_b, dt_bias):
    parts = [dx, db] if dc is None else [dx, db, dc]
    xbc = jnp.concatenate(parts, axis=-1)
    n_ch = xbc.shape[-1]
    xbc = jax.nn.silu(depthwise_conv(xbc, conv_w[:, :n_ch], conv_b[:n_ch]))
    bsz, length = dx.shape[:2]
    bc_w = SSM_GROUPS * SSM_STATE
    xs = xbc[..., :GROUP_W].reshape(bsz, length, SSM_GROUPS, SSM_HPG, SSM_HEAD_DIM)
    bm = xbc[..., GROUP_W:GROUP_W + bc_w].reshape(bsz, length, SSM_GROUPS, SSM_STATE)
    cm = None if dc is None else xbc[..., GROUP_W + bc_w:].reshape(bsz, length, SSM_GROUPS, SSM_STATE)
    dt = jax.nn.softplus(ddt.astype(jnp.float32).reshape(bsz, length, 2, SSM_GROUPS, SSM_HPG)
                         + dt_bias.astype(jnp.float32).reshape(2, SSM_GROUPS, SSM_HPG))
    return xs, bm, cm, dt


def ssd_chunked(x, dt, a, bm, cm, h0):
    bsz, length, g, r, p = x.shape
    nc = length // SSM_CHUNK

    def chunks(t):
        return t.astype(jnp.float32).reshape((bsz, nc, SSM_CHUNK) + t.shape[2:])

    xc, dtc, bc, cc = chunks(x), chunks(dt), chunks(bm), chunks(cm)
    cum = jnp.cumsum(dtc * a, axis=2)
    xdt = xc * dtc[..., None]
    tri = jnp.tril(jnp.ones((SSM_CHUNK, SSM_CHUNK), dtype=bool))
    seg = jnp.where(tri[None, None, :, :, None, None], cum[:, :, :, None] - cum[:, :, None, :], -jnp.inf)
    scores = jnp.einsum('bcign,bcjgn->bcijg', cc, bc)
    y_diag = jnp.einsum('bcijg,bcijgr,bcjgrp->bcigrp', scores, jnp.exp(seg), xdt)
    decay_end = jnp.exp(cum[:, :, -1:] - cum)
    states = jnp.einsum('bcjgn,bcjgr,bcjgrp->bcgrpn', bc, decay_end, xdt)
    chunk_decay = jnp.exp(cum[:, :, -1])

    def carry_step(h, inp):
        s, dcy = inp
        return h * dcy[..., None, None] + s, h

    h_last, h_in = lax.scan(carry_step, h0.astype(jnp.float32),
                            (jnp.moveaxis(states, 1, 0), jnp.moveaxis(chunk_decay, 1, 0)))
    h_in = jnp.moveaxis(h_in, 0, 1)
    y_off = jnp.einsum('bcign,bcgrpn,bcigr->bcigrp', cc, h_in, jnp.exp(cum))
    return (y_diag + y_off).reshape(bsz, length, g, r, p), h_last


def ssd_final_state(x, dt, a, bm):
    dt = dt.astype(jnp.float32)
    cum = jnp.cumsum(dt * a, axis=1)
    w = jnp.exp(cum[:, -1:] - cum) * dt
    return jnp.einsum('blgn,blgr,blgrp->bgrpn', bm.astype(jnp.float32), w, x.astype(jnp.float32))


def orient(t, d):
    return jnp.flip(t, axis=1) if d == 1 else t


def bidirectional_ssd(lat, ctx, a_log, d_skip, ctx_out):
    x_l, b_l, c_l, dt_l = lat
    x_c, b_c, c_c, dt_c = ctx
    h0 = jnp.zeros((x_l.shape[0], SSM_GROUPS, SSM_HPG, SSM_HEAD_DIM, SSM_STATE), jnp.float32)
    y_l, y_c = 0.0, 0.0
    for d in range(2):
        a = -jnp.exp(a_log[d].astype(jnp.float32)).reshape(SSM_GROUPS, SSM_HPG)
        skip = d_skip[d].astype(jnp.float32).reshape(SSM_GROUPS, SSM_HPG, 1)
        if ctx_out:
            yc, h_c = ssd_chunked(orient(x_c, d), orient(dt_c[:, :, d], d), a, orient(b_c, d), orient(c_c, d), h0)
            y_c = y_c + orient(yc, d) + skip * x_c.astype(jnp.float32)
        else:
            h_c = ssd_final_state(orient(x_c, d), orient(dt_c[:, :, d], d), a, orient(b_c, d))
        yl, _ = ssd_chunked(orient(x_l, d), orient(dt_l[:, :, d], d), a, orient(b_l, d), orient(c_l, d), h_c)
        y_l = y_l + orient(yl, d) + skip * x_l.astype(jnp.float32)
    return y_l, (y_c if ctx_out else None)


def merge_groups(outs, out_norm, w_out):
    normed = [rms_norm(o, out_norm[g * GROUP_W:(g + 1) * GROUP_W]) for g, o in enumerate(outs)]
    return jnp.concatenate(normed, axis=-1) @ w_out


def swiglu(h, w13, w2):
    gate, up = jnp.split(h @ w13, 2, axis=-1)
    return (jax.nn.silu(gate) * up) @ w2


def moe_swiglu(h, router, w13, w2):
    lead = h.shape[:-1]
    t = h.reshape(-1, h.shape[-1])
    n_tok = t.shape[0]
    n_assign = n_tok * TOP_K
    n_blocks = -(-n_assign // MOE_BLOCK) + N_EXPERTS
    n_slots = n_blocks * MOE_BLOCK
    logits = (t @ router).astype(jnp.float32)
    top_v, top_e = lax.top_k(logits, TOP_K)
    top_w = jax.nn.softmax(top_v, axis=-1).reshape(-1)
    flat_e = top_e.reshape(-1)
    order = jnp.argsort(flat_e * n_assign + jnp.arange(n_assign))
    sorted_e = flat_e[order]
    counts = jnp.zeros((N_EXPERTS,), jnp.int32).at[flat_e].add(1)
    padded = (counts + MOE_BLOCK - 1) // MOE_BLOCK * MOE_BLOCK
    seg_start = jnp.cumsum(counts) - counts
    pad_end = jnp.cumsum(padded)
    dest = pad_end[sorted_e] - padded[sorted_e] + jnp.arange(n_assign) - seg_start[sorted_e]
    slot_tok = jnp.zeros((n_slots,), jnp.int32).at[dest].set(order // TOP_K)
    slot_w = jnp.zeros((n_slots,), jnp.float32).at[dest].set(top_w[order])
    block_e = jnp.minimum(jnp.searchsorted(pad_end, jnp.arange(n_blocks) * MOE_BLOCK, side='right'), N_EXPERTS - 1)
    xs = t[slot_tok].reshape(n_blocks, MOE_BLOCK, t.shape[-1])

    def expert_block(args):
        e, xb = args
        return swiglu(xb, w13[e], w2[e])

    ys = lax.map(expert_block, (block_e, xs)).reshape(n_slots, -1)
    out = jnp.zeros_like(t).at[slot_tok].add(ys * slot_w[:, None].astype(ys.dtype))
    return out.reshape(lead + (t.shape[-1],))


def hybrid_layer(x, xc, c, c_ctx, rope, ffn, ctx_out, w_mod, b_mod, norm_mix, norm_ffn, w_in, q_norm, k_norm,
                 rpb, sgu_norm, sgu_w, sgu_b, conv_w, conv_b, a_log, dt_bias, d_skip, out_norm, w_out):
    cos, sin = rope
    bsz, s_len, _ = x.shape
    rep = ATT_HEADS // ATT_KV_HEADS
    sh_l, sc_l, g_l, sh2_l, sc2_l, g2_l = modulation(c, w_mod, b_mod, 6)
    mod_c = modulation(c_ctx, w_mod, b_mod, 6 if ctx_out else 2)
    h_l = rms_norm(x, norm_mix) * (1 + sc_l) + sh_l
    h_c = rms_norm(xc, norm_mix) * (1 + mod_c[1]) + mod_c[0]

    (aq_l, bq_l, cu_l, cv_l, dz_l, dc_l, ak_l, av_l, bk_l, bv_l, dx_l, db_l, ddt_l) = split_cols(h_l @ w_in, Q_SIDE + K_SIDE)
    if ctx_out:
        (aq_c, bq_c, cu_c, cv_c, dz_c, dc_c, ak_c, av_c, bk_c, bv_c, dx_c, db_c, ddt_c) = split_cols(h_c @ w_in, Q_SIDE + K_SIDE)
    else:
        (ak_c, av_c, bk_c, bv_c, dx_c, db_c, ddt_c) = split_cols(h_c @ w_in[:, Q_COLS:], K_SIDE)
        dc_c = None

    qa_l = apply_rope(rms_norm(heads(aq_l, ATT_HEADS), q_norm), cos, sin)
    ka_l = apply_rope(rms_norm(heads(ak_l, ATT_KV_HEADS), k_norm), cos, sin)
    ka_c = rms_norm(heads(ak_c, ATT_KV_HEADS), k_norm)
    va_c = heads(av_c, ATT_KV_HEADS)
    o_a_l = gqa_block_attention(qa_l.reshape(bsz, s_len, ATT_KV_HEADS, rep, HEAD_DIM),
                                jnp.concatenate([ka_c, ka_l], axis=1),
                                jnp.concatenate([va_c, heads(av_l, ATT_KV_HEADS)], axis=1))

    kb_c, vb_c = heads(bk_c, NA_HEADS), heads(bv_c, NA_HEADS)
    o_b_l = neighbourhood_attention(heads(bq_l, NA_HEADS), heads(bk_l, NA_HEADS), heads(bv_l, NA_HEADS),
                                    kb_c, vb_c, rpb)

    o_c_l = spatial_gating_unit(jax.nn.gelu(cu_l), jax.nn.gelu(cv_l), sgu_norm, sgu_w, sgu_b)

    y_l, y_c = bidirectional_ssd(ssm_inputs(dx_l, db_l, dc_l, ddt_l, conv_w, conv_b, dt_bias),
                                 ssm_inputs(dx_c, db_c, dc_c, ddt_c, conv_w, conv_b, dt_bias),
                                 a_log, d_skip, ctx_out)
    o_d_l = y_l.reshape(bsz, s_len, GROUP_W).astype(x.dtype) * jax.nn.silu(dz_l)

    x = x + g_l * merge_groups((o_a_l, o_b_l, o_c_l, o_d_l), out_norm, w_out)
    x = x + g2_l * ffn(rms_norm(x, norm_ffn) * (1 + sc2_l) + sh2_l)
    if not ctx_out:
        return x, None

    n_ctx = xc.shape[1]
    o_a_c = gqa_block_attention(rms_norm(heads(aq_c, ATT_HEADS), q_norm).reshape(bsz, n_ctx, ATT_KV_HEADS, rep, HEAD_DIM),
                                ka_c, va_c)
    o_b_c = gqa_block_attention(heads(bq_c, NA_HEADS)[:, :, :, None, :], kb_c, vb_c)
    o_c_c = spatial_gating_unit(jax.nn.gelu(cu_c), jax.nn.gelu(cv_c), sgu_norm, sgu_w, sgu_b)
    o_d_c = y_c.reshape(bsz, n_ctx, GROUP_W).astype(xc.dtype) * jax.nn.silu(dz_c)
    xc = xc + mod_c[2] * merge_groups((o_a_c, o_b_c, o_c_c, o_d_c), out_norm, w_out)
    xc = xc + mod_c[5] * ffn(rms_norm(xc, norm_ffn) * (1 + mod_c[4]) + mod_c[3])
    return x, xc


def setup_inputs(seed: int = 0) -> dict:
    key = jax.random.key(seed)
    ks = jax.random.split(key, 28)
    d = D_MODEL
    f32 = jnp.float32

    def normal(k, shape, scale):
        return scale * jax.random.normal(k, shape, f32)

    def gain(k, shape):
        return 1.0 + 0.05 * jax.random.normal(k, shape, f32)

    dt0 = jnp.exp(jax.random.uniform(ks[17], (DEPTH, 2, SSM_HEADS), f32, math.log(1e-3), math.log(1e-1)))
    return {
        'x': normal(ks[0], (BATCH, SEQ, d), 1.0),
        'c': normal(ks[1], (BATCH, d), 1.0),
        'ctx': normal(ks[2], (BATCH, CTX_LEN, d), 1.0),
        'c_ctx': normal(ks[3], (d,), 1.0),
        'w_mod': normal(ks[4], (DEPTH, d, 6 * d), 0.5 * d ** -0.5),
        'b_mod': normal(ks[5], (DEPTH, 6 * d), 0.02),
        'norm_mix': gain(ks[6], (DEPTH, d)),
        'norm_ffn': gain(ks[7], (DEPTH, d)),
        'w_in': normal(ks[8], (DEPTH, d, IN_COLS), d ** -0.5),
        'q_norm': gain(ks[9], (DEPTH, HEAD_DIM)),
        'k_norm': gain(ks[10], (DEPTH, HEAD_DIM)),
        'rpb': normal(ks[11], (DEPTH, NA_HEADS, 2 * NA_ROWS - 1, 2 * NA_COLS - 1), 0.1),
        'sgu_norm': gain(ks[12], (DEPTH, GROUP_W)),
        'sgu_w': normal(ks[13], (DEPTH, SGU_GROUPS, SGU_CHUNK, SGU_CHUNK), SGU_CHUNK ** -0.5),
        'sgu_b': 1.0 + normal(ks[14], (DEPTH, SGU_GROUPS, SGU_CHUNK), 0.1),
        'conv_w': normal(ks[15], (DEPTH, CONV_K, CONV_CH), CONV_K ** -0.5),
        'conv_b': normal(ks[16], (DEPTH, CONV_CH), 0.02),
        'a_log': jnp.log(jax.random.uniform(ks[18], (DEPTH, 2, SSM_HEADS), f32, 1.0, 16.0)),
        'dt_bias': dt0 + jnp.log(-jnp.expm1(-dt0)),
        'd_skip': gain(ks[19], (DEPTH, 2, SSM_HEADS)),
        'out_norm': gain(ks[20], (DEPTH, D_MIX)),
        'w_out': normal(ks[21], (DEPTH, D_MIX, d), D_MIX ** -0.5),
        'ffn_w13': normal(ks[22], (N_DENSE, d, 2 * D_FF), d ** -0.5),
        'ffn_w2': normal(ks[23], (N_DENSE, D_FF, d), D_FF ** -0.5),
        'router': normal(ks[24], (N_MOE, d, N_EXPERTS), d ** -0.5),
        'moe_w13': normal(ks[25], (N_MOE, N_EXPERTS, d, 2 * D_FF_EXPERT), d ** -0.5),
        'moe_w2': normal(ks[26], (N_MOE, N_EXPERTS, D_FF_EXPERT, d), D_FF_EXPERT ** -0.5),
        'final_norm': gain(ks[27], (d,)),
    }


def reference(x, c, ctx, c_ctx, w_mod, b_mod, norm_mix, norm_ffn, w_in, q_norm, k_norm, rpb, sgu_norm, sgu_w,
              sgu_b, conv_w, conv_b, a_log, dt_bias, d_skip, out_norm, w_out, ffn_w13, ffn_w2, router, moe_w13,
              moe_w2, final_norm):
    rope = axial_rope(x.shape[1])
    xc = ctx
    for i in range(DEPTH):
        if i % 2 == 0:
            ffn = functools.partial(swiglu, w13=ffn_w13[i // 2], w2=ffn_w2[i // 2])
        else:
            ffn = functools.partial(moe_swiglu, router=router[i // 2], w13=moe_w13[i // 2], w2=moe_w2[i // 2])
        x, xc = hybrid_layer(x, xc, c, c_ctx, rope, ffn, i < DEPTH - 1, w_mod[i], b_mod[i], norm_mix[i],
                             norm_ffn[i], w_in[i], q_norm[i], k_norm[i], rpb[i], sgu_norm[i], sgu_w[i], sgu_b[i],
                             conv_w[i], conv_b[i], a_log[i], dt_bias[i], d_skip[i], out_norm[i], w_out[i])
    return rms_norm(x, final_norm)
```

```python
import functools
import math

import numpy as np
import jax
import jax.numpy as jnp
from jax import lax
from jax.experimental import pallas as pl
from jax.experimental.pallas import tpu as pltpu

F32 = jnp.float32
BF16 = jnp.bfloat16

D_MODEL = 2048
DEPTH = 2
GRID_W = 64
EPS = 1e-6
GROUP_W = 512
HEAD_DIM = 128
ATT_HEADS = 4
ATT_KV_HEADS = 2
ROPE_THETA = 10000.0
Q_BLOCK = 128
NA_HEADS = 4
NA_ROWS = 8
NA_COLS = 16
SGU_CHUNK = 128
SGU_GROUPS = 4
SGU_GROUP_W = 128
SSM_HEAD_DIM = 64
SSM_HEADS = 8
SSM_GROUPS = 2
SSM_HPG = 4
SSM_STATE = 128
SSM_CHUNK = 128
CONV_K = 5
D_FF = 5632
N_EXPERTS = 8
TOP_K = 2
D_FF_EXPERT = 7168

Q_SIDE = (512, 512, 512, 512, 512, 256)
K_SIDE = (256, 256, 512, 512, 512, 256, 16)
Q_COLS = sum(Q_SIDE)
IN_COLS = Q_COLS + sum(K_SIDE)
IN_MAIN = IN_COLS - 2 * SSM_HEADS
DT_PAD = 128

VMEM_LIMIT = 52 * 1024 * 1024


def _cparams(sem):
    return pltpu.CompilerParams(dimension_semantics=sem, vmem_limit_bytes=VMEM_LIMIT)


def _silu(x):
    return x * (1.0 / (1.0 + jnp.exp(-x)))


def _norm_mod(x, g, sc, sh):
    ms = jnp.mean(x * x, axis=-1, keepdims=True)
    return (x * lax.rsqrt(ms + EPS) * g) * (1.0 + sc) + sh


def _mod_kernel(c_ref, w_ref, b_ref, o_ref):
    a = _silu(c_ref[...])
    o_ref[...] = jnp.dot(a, w_ref[...], preferred_element_type=F32,
                         precision=lax.Precision.HIGHEST) + b_ref[...]


def modulation_rows(cond8, w_mod, b_mod):
    n = w_mod.shape[1]
    tn = 1024
    return pl.pallas_call(
        _mod_kernel,
        out_shape=jax.ShapeDtypeStruct((8, n), F32),
        grid=(n // tn,),
        in_specs=[pl.BlockSpec((8, D_MODEL), lambda j: (0, 0)),
                  pl.BlockSpec((D_MODEL, tn), lambda j: (0, j)),
                  pl.BlockSpec((1, tn), lambda j: (0, j))],
        out_specs=pl.BlockSpec((8, tn), lambda j: (0, j)),
        compiler_params=_cparams(("parallel",)),
        name="modulation",
    )(cond8, w_mod, b_mod.reshape(1, n))


def _in_proj_kernel(x_ref, g_ref, sc_ref, sh_ref, w_ref, wdt_ref, o_ref, odt_ref, hn_ref):
    @pl.when(pl.program_id(2) == 0)
    def _():
        hb = _norm_mod(x_ref[0], g_ref[...], sc_ref[0], sh_ref[0]).astype(BF16)
        hn_ref[...] = hb
        odt_ref[0] = jnp.dot(hb, wdt_ref[...].astype(BF16), preferred_element_type=F32)

    o_ref[0] = jnp.dot(hn_ref[...], w_ref[...].astype(BF16), preferred_element_type=F32)


def in_proj(x, gain, sc, sh, w_in, w_dt, tm):
    b, l, d = x.shape
    tn = 512
    return pl.pallas_call(
        _in_proj_kernel,
        out_shape=(jax.ShapeDtypeStruct((b, l, IN_MAIN), F32),
                   jax.ShapeDtypeStruct((b, l, DT_PAD), F32)),
        grid=(b, l // tm, IN_MAIN // tn),
        in_specs=[pl.BlockSpec((1, tm, d), lambda bi, i, j: (bi, i, 0)),
                  pl.BlockSpec((1, d), lambda bi, i, j: (0, 0)),
                  pl.BlockSpec((1, 1, d), lambda bi, i, j: (bi, 0, 0)),
                  pl.BlockSpec((1, 1, d), lambda bi, i, j: (bi, 0, 0)),
                  pl.BlockSpec((d, tn), lambda bi, i, j: (0, j)),
                  pl.BlockSpec((d, DT_PAD), lambda bi, i, j: (0, 0))],
        out_specs=(pl.BlockSpec((1, tm, tn), lambda bi, i, j: (bi, i, j)),
                   pl.BlockSpec((1, tm, DT_PAD), lambda bi, i, j: (bi, i, 0))),
        scratch_shapes=[pltpu.VMEM((tm, d), BF16)],
        compiler_params=_cparams(("parallel", "parallel", "arbitrary")),
        name="in_proj",
    )(x, gain.reshape(1, d), sc, sh, w_in, w_dt)


def _out_proj_kernel(oa_ref, ob_ref, oc_ref, od_ref, gn_ref, w_ref, x_ref, gate_ref, o_ref, hn_ref):
    @pl.when(pl.program_id(2) == 0)
    def _():
        for k, r in enumerate((oa_ref, ob_ref, oc_ref, od_ref)):
            v = r[0]
            ms = jnp.mean(v * v, axis=-1, keepdims=True)
            y = v * lax.rsqrt(ms + EPS) * gn_ref[:, k * GROUP_W:(k + 1) * GROUP_W]
            hn_ref[:, k * GROUP_W:(k + 1) * GROUP_W] = y.astype(BF16)

    acc = jnp.dot(hn_ref[...], w_ref[...].astype(BF16), preferred_element_type=F32)
    o_ref[0] = x_ref[0] + gate_ref[0] * acc


def out_proj(oa, ob, oc, od, out_norm, w_out, x, gate, tm):
    b, l, d = x.shape
    tn = 512
    grp = pl.BlockSpec((1, tm, GROUP_W), lambda bi, i, j: (bi, i, 0))
    return pl.pallas_call(
        _out_proj_kernel,
        out_shape=jax.ShapeDtypeStruct((b, l, d), F32),
        grid=(b, l // tm, d // tn),
        in_specs=[grp, grp, grp, grp,
                  pl.BlockSpec((1, d), lambda bi, i, j: (0, 0)),
                  pl.BlockSpec((d, tn), lambda bi, i, j: (0, j)),
                  pl.BlockSpec((1, tm, tn), lambda bi, i, j: (bi, i, j)),
                  pl.BlockSpec((1, 1, tn), lambda bi, i, j: (bi, 0, j))],
        out_specs=pl.BlockSpec((1, tm, tn), lambda bi, i, j: (bi, i, j)),
        scratch_shapes=[pltpu.VMEM((tm, d), BF16)],
        compiler_params=_cparams(("parallel", "parallel", "arbitrary")),
        name="out_proj",
    )(oa, ob, oc, od, out_norm.reshape(1, d), w_out, x, gate)


def _ffn_kernel(x_ref, g_ref, sc_ref, sh_ref, wg_ref, wu_ref, w2_ref, gate_ref, o_ref, hn_ref):
    f = pl.program_id(2)

    @pl.when(f == 0)
    def _():
        hn_ref[...] = _norm_mod(x_ref[0], g_ref[...], sc_ref[0], sh_ref[0]).astype(BF16)
        o_ref[...] = jnp.zeros_like(o_ref)

    h = hn_ref[...]
    a = jnp.dot(h, wg_ref[...].astype(BF16), preferred_element_type=F32)
    u = jnp.dot(h, wu_ref[...].astype(BF16), preferred_element_type=F32)
    act = (_silu(a) * u).astype(BF16)
    o_ref[0] += jnp.dot(act, w2_ref[...].astype(BF16), preferred_element_type=F32)

    @pl.when(f == pl.num_programs(2) - 1)
    def _():
        o_ref[0] = x_ref[0] + gate_ref[0] * o_ref[0]


def dense_ffn(x, gain, sc, sh, gate, w13, w2, tm):
    b, l, d = x.shape
    ff = w2.shape[0]
    tf = 512
    nf = ff // tf
    vec = pl.BlockSpec((1, 1, d), lambda bi, i, f: (bi, 0, 0))
    return pl.pallas_call(
        _ffn_kernel,
        out_shape=jax.ShapeDtypeStruct((b, l, d), F32),
        grid=(b, l // tm, nf),
        in_specs=[pl.BlockSpec((1, tm, d), lambda bi, i, f: (bi, i, 0)),
                  pl.BlockSpec((1, d), lambda bi, i, f: (0, 0)),
                  vec, vec,
                  pl.BlockSpec((d, tf), lambda bi, i, f: (0, f)),
                  pl.BlockSpec((d, tf), lambda bi, i, f: (0, nf + f)),
                  pl.BlockSpec((tf, d), lambda bi, i, f: (f, 0)),
                  vec],
        out_specs=pl.BlockSpec((1, tm, d), lambda bi, i, f: (bi, i, 0)),
        scratch_shapes=[pltpu.VMEM((tm, d), BF16)],
        compiler_params=_cparams(("parallel", "parallel", "arbitrary")),
        name="dense_ffn",
    )(x, gain.reshape(1, d), sc, sh, w13, w13, w2, gate)


def _expert_kernel(be_ref, xs_ref, wg_ref, wu_ref, w2_ref, o_ref):
    f = pl.program_id(1)

    @pl.when(f == 0)
    def _():
        o_ref[...] = jnp.zeros_like(o_ref)

    h = xs_ref[...]
    a = jnp.dot(h, wg_ref[0].astype(BF16), preferred_element_type=F32)
    u = jnp.dot(h, wu_ref[0].astype(BF16), preferred_element_type=F32)
    act = (_silu(a) * u).astype(BF16)
    o_ref[...] += jnp.dot(act, w2_ref[0].astype(BF16), preferred_element_type=F32)


def expert_ffn(block_e, xs, w13, w2, mb):
    n_slots, d = xs.shape
    ff = w2.shape[1]
    tf = 256
    nf = ff // tf
    grid_spec = pltpu.PrefetchScalarGridSpec(
        num_scalar_prefetch=1,
        grid=(n_slots // mb, nf),
        in_specs=[pl.BlockSpec((mb, d), lambda i, f, be: (i, 0)),
                  pl.BlockSpec((1, d, tf), lambda i, f, be: (be[i], 0, f)),
                  pl.BlockSpec((1, d, tf), lambda i, f, be: (be[i], 0, nf + f)),
                  pl.BlockSpec((1, tf, d), lambda i, f, be: (be[i], f, 0))],
        out_specs=pl.BlockSpec((mb, d), lambda i, f, be: (i, 0)),
    )
    return pl.pallas_call(
        _expert_kernel,
        out_shape=jax.ShapeDtypeStruct((n_slots, d), F32),
        grid_spec=grid_spec,
        compiler_params=_cparams(("parallel", "arbitrary")),
        name="expert_ffn",
    )(block_e, xs, w13, w13, w2)


def _rms_norm(x, g):
    xf = x.astype(F32)
    y = xf * lax.rsqrt(jnp.mean(xf * xf, axis=-1, keepdims=True) + EPS)
    return (y * g.astype(F32)).astype(x.dtype)


def _heads(t, n):
    return t.reshape(t.shape[:-1] + (n, t.shape[-1] // n))


def _axial_rope(n_tokens):
    t = jnp.arange(n_tokens)
    row = (t // GRID_W).astype(F32)
    col = (t % GRID_W).astype(F32)
    n_freq = HEAD_DIM // 4
    inv = jnp.power(ROPE_THETA, -jnp.arange(n_freq, dtype=F32) / n_freq)
    ang = jnp.concatenate([row[:, None] * inv, col[:, None] * inv], axis=-1)
    return jnp.cos(ang), jnp.sin(ang)


def _apply_rope(x, cos, sin):
    xf = x.astype(F32).reshape(x.shape[:-1] + (HEAD_DIM // 2, 2))
    x0, x1 = xf[..., 0], xf[..., 1]
    cs, sn = cos[None, :, None, :], sin[None, :, None, :]
    out = jnp.stack([x0 * cs - x1 * sn, x0 * sn + x1 * cs], axis=-1)
    return out.reshape(x.shape).astype(x.dtype)


def _gqa_block_attention(q, k, v):
    bsz, lq, g, r, dh = q.shape
    nb = lq // Q_BLOCK
    qb = jnp.moveaxis(q.reshape(bsz, nb, Q_BLOCK, g, r, dh), 1, 0)
    scale = dh ** -0.5

    def one_block(qi):
        s = jnp.einsum('bqgrd,bkgd->bgrqk', qi, k).astype(F32) * scale
        p = jax.nn.softmax(s, axis=-1).astype(v.dtype)
        return jnp.einsum('bgrqk,bkgd->bqgrd', p, v)

    o = lax.map(one_block, qb)
    return jnp.moveaxis(o, 0, 1).reshape(bsz, lq, g * r * dh)


def _neighbourhood_attention(q, k, v, k_ctx, v_ctx, rpb):
    bsz, s_len, n_h, dh = q.shape
    rows = s_len // GRID_W
    kh = min(NA_ROWS, rows)
    n_cb = GRID_W // NA_COLS
    band = 2 * NA_COLS
    n_loc = kh * band
    scale = dh ** -0.5
    qcol = jnp.arange(GRID_W).reshape(n_cb, NA_COLS)
    win_c0 = jnp.clip(qcol - NA_COLS // 2, 0, GRID_W - NA_COLS)
    band_c0 = jnp.clip(jnp.arange(n_cb) * NA_COLS - NA_COLS // 2, 0, GRID_W - band)
    kcol = band_c0[:, None] + jnp.arange(band)
    col_ok = (kcol[:, None, :] >= win_c0[:, :, None]) & (kcol[:, None, :] < win_c0[:, :, None] + NA_COLS)
    mask = jnp.broadcast_to(col_ok[:, :, None, :], (n_cb, NA_COLS, kh, band)).reshape(n_cb, NA_COLS, n_loc)
    dcol = jnp.clip(kcol[:, None, :] - qcol[:, :, None], 1 - NA_COLS, NA_COLS - 1) + NA_COLS - 1
    rpb32 = rpb.astype(F32)
    q_rows = jnp.moveaxis(q.reshape(bsz, rows, GRID_W, n_h, dh), 1, 0)

    def one_row(args):
        r, q_r = args
        krow = jnp.clip(r - kh // 2, 0, rows - kh) + jnp.arange(kh)
        idx = (krow[None, :, None] * GRID_W + kcol[:, None, :]).reshape(n_cb, n_loc)
        kb = jnp.take(k, idx, axis=1)
        vb = jnp.take(v, idx, axis=1)
        drow = krow - r + NA_ROWS - 1
        bias = rpb32[:, drow[None, None, :, None], dcol[:, :, None, :]].reshape(n_h, n_cb, NA_COLS, n_loc)
        qb = q_r.reshape(bsz, n_cb, NA_COLS, n_h, dh)
        s_loc = jnp.einsum('bnqhd,bnkhd->bhnqk', qb, kb).astype(F32) * scale + bias
        s_loc = jnp.where(mask, s_loc, -1e30)
        s_ctx = jnp.einsum('bnqhd,bchd->bhnqc', qb, k_ctx).astype(F32) * scale
        p = jax.nn.softmax(jnp.concatenate([s_loc, s_ctx], axis=-1), axis=-1).astype(v.dtype)
        o = (jnp.einsum('bhnqk,bnkhd->bnqhd', p[..., :n_loc], vb)
             + jnp.einsum('bhnqc,bchd->bnqhd', p[..., n_loc:], v_ctx))
        return o.reshape(bsz, GRID_W, n_h, dh)

    o = lax.map(one_row, (jnp.arange(rows), q_rows))
    return jnp.moveaxis(o, 0, 1).reshape(bsz, s_len, n_h * dh)


def _spatial_gating_unit(u, v, norm_g, w_s, b_s):
    bsz, length, _ = v.shape
    n = length // SGU_CHUNK
    vn = _rms_norm(v, norm_g).reshape(bsz, n, SGU_CHUNK, SGU_GROUPS, SGU_GROUP_W)
    mixed = jnp.einsum('gij,bnjgc->bnigc', w_s, vn) + b_s.T[:, :, None]
    return u * mixed.reshape(bsz, length, GROUP_W)


def _depthwise_conv(t, w, b):
    y = lax.conv_general_dilated(t, w[:, None, :].astype(t.dtype), window_strides=(1,),
                                 padding=[(CONV_K // 2, CONV_K // 2)],
                                 dimension_numbers=('NWC', 'WIO', 'NWC'),
                                 feature_group_count=t.shape[-1])
    return y + b


def _ssm_inputs(dx, db, dc, ddt, conv_w, conv_b, dt_bias):
    parts = [dx, db] if dc is None else [dx, db, dc]
    xbc = jnp.concatenate(parts, axis=-1)
    n_ch = xbc.shape[-1]
    xbc = jax.nn.silu(_depthwise_conv(xbc, conv_w[:, :n_ch], conv_b[:n_ch]))
    bsz, length = dx.shape[:2]
    bc_w = SSM_GROUPS * SSM_STATE
    xs = xbc[..., :GROUP_W].reshape(bsz, length, SSM_GROUPS, SSM_HPG, SSM_HEAD_DIM)
    bm = xbc[..., GROUP_W:GROUP_W + bc_w].reshape(bsz, length, SSM_GROUPS, SSM_STATE)
    cm = None if dc is None else xbc[..., GROUP_W + bc_w:].reshape(bsz, length, SSM_GROUPS, SSM_STATE)
    dt = jax.nn.softplus(ddt.astype(F32).reshape(bsz, length, 2, SSM_GROUPS, SSM_HPG)
                         + dt_bias.astype(F32).reshape(2, SSM_GROUPS, SSM_HPG))
    return xs, bm, cm, dt


def _ssd_chunked(x, dt, a, bm, cm, h0):
    bsz, length, g, r, p = x.shape
    nc = length // SSM_CHUNK

    def chunks(t):
        return t.astype(F32).reshape((bsz, nc, SSM_CHUNK) + t.shape[2:])

    xc, dtc, bc, cc = chunks(x), chunks(dt), chunks(bm), chunks(cm)
    cum = jnp.cumsum(dtc * a, axis=2)
    xdt = xc * dtc[..., None]
    tri = jnp.tril(jnp.ones((SSM_CHUNK, SSM_CHUNK), dtype=bool))
    seg = jnp.where(tri[None, None, :, :, None, None], cum[:, :, :, None] - cum[:, :, None, :], -jnp.inf)
    scores = jnp.einsum('bcign,bcjgn->bcijg', cc, bc)
    y_diag = jnp.einsum('bcijg,bcijgr,bcjgrp->bcigrp', scores, jnp.exp(seg), xdt)
    decay_end = jnp.exp(cum[:, :, -1:] - cum)
    states = jnp.einsum('bcjgn,bcjgr,bcjgrp->bcgrpn', bc, decay_end, xdt)
    chunk_decay = jnp.exp(cum[:, :, -1])

    def carry_step(h, inp):
        s, dcy = inp
        return h * dcy[..., None, None] + s, h

    h_last, h_in = lax.scan(carry_step, h0.astype(F32),
                            (jnp.moveaxis(states, 1, 0), jnp.moveaxis(chunk_decay, 1, 0)))
    h_in = jnp.moveaxis(h_in, 0, 1)
    y_off = jnp.einsum('bcign,bcgrpn,bcigr->bcigrp', cc, h_in, jnp.exp(cum))
    return (y_diag + y_off).reshape(bsz, length, g, r, p), h_last


def _ssd_final_state(x, dt, a, bm):
    dt = dt.astype(F32)
    cum = jnp.cumsum(dt * a, axis=1)
    w = jnp.exp(cum[:, -1:] - cum) * dt
    return jnp.einsum('blgn,blgr,blgrp->bgrpn', bm.astype(F32), w, x.astype(F32))


def _orient(t, d):
    return jnp.flip(t, axis=1) if d == 1 else t


def _bidirectional_ssd(lat, ctx, a_log, d_skip, ctx_out):
    x_l, b_l, c_l, dt_l = lat
    x_c, b_c, c_c, dt_c = ctx
    h0 = jnp.zeros((x_l.shape[0], SSM_GROUPS, SSM_HPG, SSM_HEAD_DIM, SSM_STATE), F32)
    y_l, y_c = 0.0, 0.0
    for d in range(2):
        a = -jnp.exp(a_log[d].astype(F32)).reshape(SSM_GROUPS, SSM_HPG)
        skip = d_skip[d].astype(F32).reshape(SSM_GROUPS, SSM_HPG, 1)
        if ctx_out:
            yc, h_c = _ssd_chunked(_orient(x_c, d), _orient(dt_c[:, :, d], d), a, _orient(b_c, d), _orient(c_c, d), h0)
            y_c = y_c + _orient(yc, d) + skip * x_c.astype(F32)
        else:
            h_c = _ssd_final_state(_orient(x_c, d), _orient(dt_c[:, :, d], d), a, _orient(b_c, d))
        yl, _ = _ssd_chunked(_orient(x_l, d), _orient(dt_l[:, :, d], d), a, _orient(b_l, d), _orient(c_l, d), h_c)
        y_l = y_l + _orient(yl, d) + skip * x_l.astype(F32)
    return y_l, (y_c if ctx_out else None)


MOE_MB = 1024


def _moe(x, h, gate, router, w13, w2):
    bsz, l, d = h.shape
    t = h.reshape(-1, d)
    n_tok = t.shape[0]
    n_assign = n_tok * TOP_K
    n_blocks = -(-n_assign // MOE_MB) + N_EXPERTS
    n_slots = n_blocks * MOE_MB
    logits = jnp.dot(t, router, precision=lax.Precision.HIGHEST).astype(F32)
    top_v, top_e = lax.top_k(logits, TOP_K)
    top_w = jax.nn.softmax(top_v, axis=-1).reshape(-1)
    flat_e = top_e.reshape(-1)
    order = jnp.argsort(flat_e * n_assign + jnp.arange(n_assign))
    sorted_e = flat_e[order]
    counts = jnp.zeros((N_EXPERTS,), jnp.int32).at[flat_e].add(1)
    padded = (counts + MOE_MB - 1) // MOE_MB * MOE_MB
    seg_start = jnp.cumsum(counts) - counts
    pad_end = jnp.cumsum(padded)
    dest = pad_end[sorted_e] - padded[sorted_e] + jnp.arange(n_assign) - seg_start[sorted_e]
    slot_tok = jnp.zeros((n_slots,), jnp.int32).at[dest].set((order // TOP_K).astype(jnp.int32))
    slot_w = jnp.zeros((n_slots,), F32).at[dest].set(top_w[order])
    block_e = jnp.minimum(jnp.searchsorted(pad_end, jnp.arange(n_blocks) * MOE_MB, side='right'),
                          N_EXPERTS - 1).astype(jnp.int32)
    xs = t[slot_tok].astype(BF16)
    ys = expert_ffn(block_e, xs, w13, w2, MOE_MB)
    out = jnp.zeros_like(t).at[slot_tok].add(ys * slot_w[:, None])
    return x + gate * out.reshape(bsz, l, d)


def _mixers(p_l, dt_l, p_c, dt_c, rope, ctx_out, q_norm, k_norm, rpb, sgu_norm, sgu_w, sgu_b, conv_w, conv_b,
            a_log, dt_bias, d_skip):
    cos, sin = rope
    bsz, s_len, _ = p_l.shape
    rep = ATT_HEADS // ATT_KV_HEADS
    cuts = [int(v) for v in np.cumsum(Q_SIDE + K_SIDE[:-1])[:-1]]
    aq_l, bq_l, cu_l, cv_l, dz_l, dc_l, ak_l, av_l, bk_l, bv_l, dx_l, db_l = jnp.split(p_l, cuts, axis=-1)
    aq_c, bq_c, cu_c, cv_c, dz_c, dc_c, ak_c, av_c, bk_c, bv_c, dx_c, db_c = jnp.split(p_c, cuts, axis=-1)
    ddt_l = dt_l[..., :2 * SSM_HEADS]
    ddt_c = dt_c[..., :2 * SSM_HEADS]
    if not ctx_out:
        dc_c = None

    qa_l = _apply_rope(_rms_norm(_heads(aq_l, ATT_HEADS), q_norm), cos, sin)
    ka_l = _apply_rope(_rms_norm(_heads(ak_l, ATT_KV_HEADS), k_norm), cos, sin)
    ka_c = _rms_norm(_heads(ak_c, ATT_KV_HEADS), k_norm)
    va_c = _heads(av_c, ATT_KV_HEADS)
    o_a_l = _gqa_block_attention(qa_l.reshape(bsz, s_len, ATT_KV_HEADS, rep, HEAD_DIM),
                                 jnp.concatenate([ka_c, ka_l], axis=1),
                                 jnp.concatenate([va_c, _heads(av_l, ATT_KV_HEADS)], axis=1))
    kb_c, vb_c = _heads(bk_c, NA_HEADS), _heads(bv_c, NA_HEADS)
    o_b_l = _neighbourhood_attention(_heads(bq_l, NA_HEADS), _heads(bk_l, NA_HEADS), _heads(bv_l, NA_HEADS),
                                     kb_c, vb_c, rpb)
    o_c_l = _spatial_gating_unit(jax.nn.gelu(cu_l), jax.nn.gelu(cv_l), sgu_norm, sgu_w, sgu_b)
    y_l, y_c = _bidirectional_ssd(_ssm_inputs(dx_l, db_l, dc_l, ddt_l, conv_w, conv_b, dt_bias),
                                  _ssm_inputs(dx_c, db_c, dc_c, ddt_c, conv_w, conv_b, dt_bias),
                                  a_log, d_skip, ctx_out)
    o_d_l = y_l.reshape(bsz, s_len, GROUP_W) * jax.nn.silu(dz_l)
    lat = (o_a_l, o_b_l, o_c_l, o_d_l)
    if not ctx_out:
        return lat, None
    n_ctx = p_c.shape[1]
    o_a_c = _gqa_block_attention(_rms_norm(_heads(aq_c, ATT_HEADS), q_norm).reshape(bsz, n_ctx, ATT_KV_HEADS, rep, HEAD_DIM),
                                 ka_c, va_c)
    o_b_c = _gqa_block_attention(_heads(bq_c, NA_HEADS)[:, :, :, None, :], kb_c, vb_c)
    o_c_c = _spatial_gating_unit(jax.nn.gelu(cu_c), jax.nn.gelu(cv_c), sgu_norm, sgu_w, sgu_b)
    o_d_c = y_c.reshape(bsz, n_ctx, GROUP_W) * jax.nn.silu(dz_c)
    return lat, (o_a_c, o_b_c, o_c_c, o_d_c)


def kernel(x, c, ctx, c_ctx, w_mod, b_mod, norm_mix, norm_ffn, w_in, q_norm, k_norm, rpb, sgu_norm, sgu_w, sgu_b,
           conv_w, conv_b, a_log, dt_bias, d_skip, out_norm, w_out, ffn_w13, ffn_w2, router, moe_w13, moe_w2,
           final_norm):
    bsz, s_len, d = x.shape
    n_ctx = ctx.shape[1]
    rope = _axial_rope(s_len)
    cond8 = jnp.concatenate([c, c_ctx[None, :], jnp.zeros((8 - bsz - 1, d), F32)], axis=0)
    xc = ctx
    for i in range(DEPTH):
        ctx_out = i < DEPTH - 1
        m = modulation_rows(cond8, w_mod[i], b_mod[i])
        ml = [m[:bsz, k * d:(k + 1) * d][:, None, :] for k in range(6)]
        mc = [jnp.broadcast_to(m[bsz, k * d:(k + 1) * d][None, None, :], (bsz, 1, d)) for k in range(6)]
        w_dt = jnp.pad(w_in[i][:, IN_MAIN:], ((0, 0), (0, DT_PAD - 2 * SSM_HEADS)))
        p_l, dt_l = in_proj(x, norm_mix[i], ml[1], ml[0], w_in[i], w_dt, 1024)
        p_c, dt_c = in_proj(xc, norm_mix[i], mc[1], mc[0], w_in[i], w_dt, n_ctx)
        lat, cx = _mixers(p_l, dt_l, p_c, dt_c, rope, ctx_out, q_norm[i], k_norm[i], rpb[i], sgu_norm[i], sgu_w[i],
                          sgu_b[i], conv_w[i], conv_b[i], a_log[i], dt_bias[i], d_skip[i])
        x = out_proj(*lat, out_norm[i], w_out[i], x, ml[2], 1024)
        if i % 2 == 0:
            x = dense_ffn(x, norm_ffn[i], ml[4], ml[3], ml[5], ffn_w13[i // 2], ffn_w2[i // 2], 512)
        else:
            h = _rms_norm(x, norm_ffn[i]) * (1 + ml[4]) + ml[3]
            x = _moe(x, h, ml[5], router[i // 2], moe_w13[i // 2], moe_w2[i // 2])
        if ctx_out:
            xc = out_proj(*cx, out_norm[i], w_out[i], xc, mc[2], n_ctx)
            if i % 2 == 0:
                xc = dense_ffn(xc, norm_ffn[i], mc[4], mc[3], mc[5], ffn_w13[i // 2], ffn_w2[i // 2], n_ctx)
            else:
                h = _rms_norm(xc, norm_ffn[i]) * (1 + mc[4]) + mc[3]
                xc = _moe(xc, h, mc[5], router[i // 2], moe_w13[i // 2], moe_w2[i // 2])
    return _rms_norm(x, final_norm)
```

```python
import functools
import math

import numpy as np
import jax
import jax.numpy as jnp
from jax import lax
from jax.experimental import pallas as pl
from jax.experimental.pallas import tpu as pltpu

F32 = jnp.float32
BF16 = jnp.bfloat16

D_MODEL = 2048
DEPTH = 2
GRID_W = 64
EPS = 1e-6
GROUP_W = 512
HEAD_DIM = 128
ATT_HEADS = 4
ATT_KV_HEADS = 2
ROPE_THETA = 10000.0
Q_BLOCK = 128
NA_HEADS = 4
NA_ROWS = 8
NA_COLS = 16
SGU_CHUNK = 128
SGU_GROUPS = 4
SGU_GROUP_W = 128
SSM_HEAD_DIM = 64
SSM_HEADS = 8
SSM_GROUPS = 2
SSM_HPG = 4
SSM_STATE = 128
SSM_CHUNK = 128
CONV_K = 5
D_FF = 5632
N_EXPERTS = 8
TOP_K = 2
D_FF_EXPERT = 7168

Q_SIDE = (512, 512, 512, 512, 512, 256)
K_SIDE = (256, 256, 512, 512, 512, 256, 16)
Q_COLS = sum(Q_SIDE)
IN_COLS = Q_COLS + sum(K_SIDE)
IN_MAIN = IN_COLS - 2 * SSM_HEADS
DT_PAD = 128

VMEM_LIMIT = 52 * 1024 * 1024


def _cparams(sem):
    return pltpu.CompilerParams(dimension_semantics=sem, vmem_limit_bytes=VMEM_LIMIT)


def _silu(x):
    return x * (1.0 / (1.0 + jnp.exp(-x)))


def _norm_mod(x, g, sc, sh):
    ms = jnp.mean(x * x, axis=-1, keepdims=True)
    return (x * lax.rsqrt(ms + EPS) * g) * (1.0 + sc) + sh


def _mod_kernel(c_ref, w_ref, b_ref, o_ref):
    a = _silu(c_ref[...])
    o_ref[...] = jnp.dot(a, w_ref[...], preferred_element_type=F32,
                         precision=lax.Precision.HIGHEST) + b_ref[...]


def modulation_rows(cond8, w_mod, b_mod):
    n = w_mod.shape[1]
    tn = 1024
    return pl.pallas_call(
        _mod_kernel,
        out_shape=jax.ShapeDtypeStruct((8, n), F32),
        grid=(n // tn,),
        in_specs=[pl.BlockSpec((8, D_MODEL), lambda j: (0, 0)),
                  pl.BlockSpec((D_MODEL, tn), lambda j: (0, j)),
                  pl.BlockSpec((1, tn), lambda j: (0, j))],
        out_specs=pl.BlockSpec((8, tn), lambda j: (0, j)),
        compiler_params=_cparams(("parallel",)),
        name="modulation",
    )(cond8, w_mod, b_mod.reshape(1, n))


def _in_proj_kernel(x_ref, g_ref, sc_ref, sh_ref, w_ref, wdt_ref, o_ref, odt_ref, hn_ref):
    @pl.when(pl.program_id(2) == 0)
    def _():
        hb = _norm_mod(x_ref[0], g_ref[...], sc_ref[0], sh_ref[0]).astype(BF16)
        hn_ref[...] = hb
        odt_ref[0] = jnp.dot(hb, wdt_ref[...].astype(BF16), preferred_element_type=F32)

    o_ref[0] = jnp.dot(hn_ref[...], w_ref[...].astype(BF16), preferred_element_type=F32)


def in_proj(x, gain, sc, sh, w_in, w_dt, tm):
    b, l, d = x.shape
    tn = 512
    return pl.pallas_call(
        _in_proj_kernel,
        out_shape=(jax.ShapeDtypeStruct((b, l, IN_MAIN), F32),
                   jax.ShapeDtypeStruct((b, l, DT_PAD), F32)),
        grid=(b, l // tm, IN_MAIN // tn),
        in_specs=[pl.BlockSpec((1, tm, d), lambda bi, i, j: (bi, i, 0)),
                  pl.BlockSpec((1, d), lambda bi, i, j: (0, 0)),
                  pl.BlockSpec((1, 1, d), lambda bi, i, j: (bi, 0, 0)),
                  pl.BlockSpec((1, 1, d), lambda bi, i, j: (bi, 0, 0)),
                  pl.BlockSpec((d, tn), lambda bi, i, j: (0, j)),
                  pl.BlockSpec((d, DT_PAD), lambda bi, i, j: (0, 0))],
        out_specs=(pl.BlockSpec((1, tm, tn), lambda bi, i, j: (bi, i, j)),
                   pl.BlockSpec((1, tm, DT_PAD), lambda bi, i, j: (bi, i, 0))),
        scratch_shapes=[pltpu.VMEM((tm, d), BF16)],
        compiler_params=_cparams(("parallel", "parallel", "arbitrary")),
        name="in_proj",
    )(x, gain.reshape(1, d), sc, sh, w_in, w_dt)


def _out_proj_kernel(oa_ref, ob_ref, oc_ref, od_ref, gn_ref, w_ref, x_ref, gate_ref, o_ref, hn_ref):
    @pl.when(pl.program_id(2) == 0)
    def _():
        for k, r in enumerate((oa_ref, ob_ref, oc_ref, od_ref)):
            v = r[0]
            ms = jnp.mean(v * v, axis=-1, keepdims=True)
            y = v * lax.rsqrt(ms + EPS) * gn_ref[:, k * GROUP_W:(k + 1) * GROUP_W]
            hn_ref[:, k * GROUP_W:(k + 1) * GROUP_W] = y.astype(BF16)

    acc = jnp.dot(hn_ref[...], w_ref[...].astype(BF16), preferred_element_type=F32)
    o_ref[0] = x_ref[0] + gate_ref[0] * acc


def out_proj(oa, ob, oc, od, out_norm, w_out, x, gate, tm):
    b, l, d = x.shape
    tn = 512
    grp = pl.BlockSpec((1, tm, GROUP_W), lambda bi, i, j: (bi, i, 0))
    return pl.pallas_call(
        _out_proj_kernel,
        out_shape=jax.ShapeDtypeStruct((b, l, d), F32),
        grid=(b, l // tm, d // tn),
        in_specs=[grp, grp, grp, grp,
                  pl.BlockSpec((1, d), lambda bi, i, j: (0, 0)),
                  pl.BlockSpec((d, tn), lambda bi, i, j: (0, j)),
                  pl.BlockSpec((1, tm, tn), lambda bi, i, j: (bi, i, j)),
                  pl.BlockSpec((1, 1, tn), lambda bi, i, j: (bi, 0, j))],
        out_specs=pl.BlockSpec((1, tm, tn), lambda bi, i, j: (bi, i, j)),
        scratch_shapes=[pltpu.VMEM((tm, d), BF16)],
        compiler_params=_cparams(("parallel", "parallel", "arbitrary")),
        name="out_proj",
    )(oa, ob, oc, od, out_norm.reshape(1, d), w_out, x, gate)


def _ffn_kernel(x_ref, g_ref, sc_ref, sh_ref, wg_ref, wu_ref, w2_ref, gate_ref, o_ref, hn_ref):
    f = pl.program_id(2)

    @pl.when(f == 0)
    def _():
        hn_ref[...] = _norm_mod(x_ref[0], g_ref[...], sc_ref[0], sh_ref[0]).astype(BF16)
        o_ref[...] = jnp.zeros_like(o_ref)

    h = hn_ref[...]
    a = jnp.dot(h, wg_ref[...].astype(BF16), preferred_element_type=F32)
    u = jnp.dot(h, wu_ref[...].astype(BF16), preferred_element_type=F32)
    act = (_silu(a) * u).astype(BF16)
    o_ref[0] += jnp.dot(act, w2_ref[...].astype(BF16), preferred_element_type=F32)

    @pl.when(f == pl.num_programs(2) - 1)
    def _():
        o_ref[0] = x_ref[0] + gate_ref[0] * o_ref[0]


def dense_ffn(x, gain, sc, sh, gate, w13, w2, tm):
    b, l, d = x.shape
    ff = w2.shape[0]
    tf = 512
    nf = ff // tf
    vec = pl.BlockSpec((1, 1, d), lambda bi, i, f: (bi, 0, 0))
    return pl.pallas_call(
        _ffn_kernel,
        out_shape=jax.ShapeDtypeStruct((b, l, d), F32),
        grid=(b, l // tm, nf),
        in_specs=[pl.BlockSpec((1, tm, d), lambda bi, i, f: (bi, i, 0)),
                  pl.BlockSpec((1, d), lambda bi, i, f: (0, 0)),
                  vec, vec,
                  pl.BlockSpec((d, tf), lambda bi, i, f: (0, f)),
                  pl.BlockSpec((d, tf), lambda bi, i, f: (0, nf + f)),
                  pl.BlockSpec((tf, d), lambda bi, i, f: (f, 0)),
                  vec],
        out_specs=pl.BlockSpec((1, tm, d), lambda bi, i, f: (bi, i, 0)),
        scratch_shapes=[pltpu.VMEM((tm, d), BF16)],
        compiler_params=_cparams(("parallel", "parallel", "arbitrary")),
        name="dense_ffn",
    )(x, gain.reshape(1, d), sc, sh, w13, w13, w2, gate)


def _expert_kernel(be_ref, xs_ref, wg_ref, wu_ref, w2_ref, o_ref):
    f = pl.program_id(1)

    @pl.when(f == 0)
    def _():
        o_ref[...] = jnp.zeros_like(o_ref)

    h = xs_ref[...]
    a = jnp.dot(h, wg_ref[0].astype(BF16), preferred_element_type=F32)
    u = jnp.dot(h, wu_ref[0].astype(BF16), preferred_element_type=F32)
    act = (_silu(a) * u).astype(BF16)
    o_ref[...] += jnp.dot(act, w2_ref[0].astype(BF16), preferred_element_type=F32)


def expert_ffn(block_e, xs, w13, w2, mb):
    n_slots, d = xs.shape
    ff = w2.shape[1]
    tf = 256
    nf = ff // tf
    grid_spec = pltpu.PrefetchScalarGridSpec(
        num_scalar_prefetch=1,
        grid=(n_slots // mb, nf),
        in_specs=[pl.BlockSpec((mb, d), lambda i, f, be: (i, 0)),
                  pl.BlockSpec((1, d, tf), lambda i, f, be: (be[i], 0, f)),
                  pl.BlockSpec((1, d, tf), lambda i, f, be: (be[i], 0, nf + f)),
                  pl.BlockSpec((1, tf, d), lambda i, f, be: (be[i], f, 0))],
        out_specs=pl.BlockSpec((mb, d), lambda i, f, be: (i, 0)),
    )
    return pl.pallas_call(
        _expert_kernel,
        out_shape=jax.ShapeDtypeStruct((n_slots, d), F32),
        grid_spec=grid_spec,
        compiler_params=_cparams(("parallel", "arbitrary")),
        name="expert_ffn",
    )(block_e, xs, w13, w13, w2)


GRID_ROWS = 64
NA_SPAN = NA_ROWS * GRID_W
COL_BQ = Q_SIDE[0] // HEAD_DIM
COL_BK = (Q_COLS + 2 * ATT_KV_HEADS * HEAD_DIM) // HEAD_DIM
COL_BV = COL_BK + NA_HEADS


def na_bias_table(rpb):
    q = np.arange(GRID_W)[:, None]
    kc = np.arange(GRID_W)[None, :]
    win = np.clip(q - NA_COLS // 2, 0, GRID_W - NA_COLS)
    ok = (kc >= win) & (kc < win + NA_COLS)
    dcol = np.clip(kc - q, 1 - NA_COLS, NA_COLS - 1) + NA_COLS - 1
    drow = np.arange(NA_ROWS)[:, None] + np.arange(NA_ROWS)[None, :]
    t = rpb.astype(F32)[:, drow][:, :, :, dcol]
    t = jnp.where(ok[None, None, None], t, -1e30)
    t = jnp.transpose(t, (1, 0, 3, 2, 4))
    return t.reshape(NA_ROWS, NA_HEADS, GRID_W, NA_SPAN)


def _na_kernel(q_ref, k_ref, v_ref, kc_ref, vc_ref, bias_ref, o_ref, ks_ref, vs_ref):
    ks_ref[...] = k_ref[0].astype(BF16)
    vs_ref[...] = v_ref[0].astype(BF16)
    kc = kc_ref[0].astype(BF16)
    vc = vc_ref[0].astype(BF16)
    scale = HEAD_DIM ** -0.5
    nt = (((1,), (1,)), ((), ()))

    def row(r, carry):
        k0 = jnp.clip(r - NA_ROWS // 2, 0, GRID_ROWS - NA_ROWS)
        d0 = k0 - r + NA_ROWS - 1
        qs = pl.multiple_of(r * GRID_W, GRID_W)
        ksrt = pl.multiple_of(k0 * GRID_W, GRID_W)
        q = (q_ref[0, pl.ds(qs, GRID_W), :] * scale).astype(BF16)
        kb = ks_ref[pl.ds(ksrt, NA_SPAN), :]
        vb = vs_ref[pl.ds(ksrt, NA_SPAN), :]
        s_loc = lax.dot_general(q, kb, nt, preferred_element_type=F32) + bias_ref[d0, 0]
        s_ctx = lax.dot_general(q, kc, nt, preferred_element_type=F32)
        m = jnp.maximum(jnp.max(s_loc, axis=-1, keepdims=True), jnp.max(s_ctx, axis=-1, keepdims=True))
        p_loc = jnp.exp(s_loc - m)
        p_ctx = jnp.exp(s_ctx - m)
        l = jnp.sum(p_loc, axis=-1, keepdims=True) + jnp.sum(p_ctx, axis=-1, keepdims=True)
        o = (jnp.dot(p_loc.astype(BF16), vb, preferred_element_type=F32)
             + jnp.dot(p_ctx.astype(BF16), vc, preferred_element_type=F32))
        o_ref[0, pl.ds(qs, GRID_W), :] = o / l
        return carry

    lax.fori_loop(0, GRID_ROWS, row, 0)


def neighbourhood_attention(p_l, p_c, bias):
    b, s, _ = p_l.shape
    n_ctx = p_c.shape[1]

    def col(c0):
        return lambda bi, h: (bi, 0, c0 + h)

    return pl.pallas_call(
        _na_kernel,
        out_shape=jax.ShapeDtypeStruct((b, s, GROUP_W), F32),
        grid=(b, NA_HEADS),
        in_specs=[pl.BlockSpec((1, s, HEAD_DIM), col(COL_BQ)),
                  pl.BlockSpec((1, s, HEAD_DIM), col(COL_BK)),
                  pl.BlockSpec((1, s, HEAD_DIM), col(COL_BV)),
                  pl.BlockSpec((1, n_ctx, HEAD_DIM), col(COL_BK)),
                  pl.BlockSpec((1, n_ctx, HEAD_DIM), col(COL_BV)),
                  pl.BlockSpec((NA_ROWS, 1, GRID_W, NA_SPAN), lambda bi, h: (0, h, 0, 0))],
        out_specs=pl.BlockSpec((1, s, HEAD_DIM), lambda bi, h: (bi, 0, h)),
        scratch_shapes=[pltpu.VMEM((s, HEAD_DIM), BF16), pltpu.VMEM((s, HEAD_DIM), BF16)],
        compiler_params=_cparams(("parallel", "parallel")),
        name="neighbourhood_attention",
    )(p_l, p_l, p_l, p_c, p_c, bias)


COL_AK = Q_COLS // HEAD_DIM
COL_AV = COL_AK + ATT_KV_HEADS
ATT_REP = ATT_HEADS // ATT_KV_HEADS
ATT_TQ = 256
ATT_TK = 512
NT_DIMS = (((1,), (1,)), ((), ()))


def rope_tables(n_tokens):
    t = jnp.arange(n_tokens)
    row = (t // GRID_W).astype(F32)
    col = (t % GRID_W).astype(F32)
    n_freq = HEAD_DIM // 4
    inv = jnp.power(ROPE_THETA, -jnp.arange(n_freq, dtype=F32) / n_freq)
    ang = jnp.concatenate([row[:, None] * inv, col[:, None] * inv], axis=-1)
    cos, sin = jnp.cos(ang), jnp.sin(ang)
    cos_i = jnp.repeat(cos, 2, axis=-1)
    sin_i = jnp.stack([-sin, sin], axis=-1).reshape(n_tokens, HEAD_DIM)
    return cos_i, sin_i


def _head_rms(x, g):
    return x * lax.rsqrt(jnp.mean(x * x, axis=-1, keepdims=True) + EPS) * g


def _rope(x, cos_i, sin_i):
    lane = lax.broadcasted_iota(jnp.int32, x.shape, 1)
    swapped = jnp.where(lane % 2 == 0, pltpu.roll(x, HEAD_DIM - 1, 1), pltpu.roll(x, 1, 1))
    return x * cos_i + swapped * sin_i


def _attn_kernel(q_ref, k_ref, v_ref, kc_ref, vc_ref, qn_ref, kn_ref, cosq_ref, sinq_ref, cosk_ref, sink_ref,
                 o_ref, ks_ref, vs_ref):
    s_len = k_ref.shape[1]
    n_ctx = kc_ref.shape[1]

    @pl.when(pl.program_id(2) == 0)
    def _():
        kn = _head_rms(k_ref[0], kn_ref[...])
        ks_ref[0:s_len, :] = _rope(kn, cosk_ref[...], sink_ref[...]).astype(BF16)
        vs_ref[0:s_len, :] = v_ref[0].astype(BF16)
        ks_ref[s_len:s_len + n_ctx, :] = _head_rms(kc_ref[0], kn_ref[...]).astype(BF16)
        vs_ref[s_len:s_len + n_ctx, :] = vc_ref[0].astype(BF16)

    scale = HEAD_DIM ** -0.5
    qh = []
    for r in range(ATT_REP):
        qn = _head_rms(q_ref[0, :, r * HEAD_DIM:(r + 1) * HEAD_DIM], qn_ref[...])
        qh.append((_rope(qn, cosq_ref[...], sinq_ref[...]) * scale).astype(BF16))
    q2 = jnp.concatenate(qh, axis=0)
    rows = q2.shape[0]

    def block(kb, vb, carry):
        m, l, acc = carry
        s = lax.dot_general(q2, kb, NT_DIMS, preferred_element_type=F32)
        m_new = jnp.maximum(m, jnp.max(s, axis=-1, keepdims=True))
        alpha = jnp.exp(m - m_new)
        p = jnp.exp(s - m_new)
        l = alpha * l + jnp.sum(p, axis=-1, keepdims=True)
        acc = alpha * acc + jnp.dot(p.astype(BF16), vb, preferred_element_type=F32)
        return m_new, l, acc

    def step(j, carry):
        st = pl.multiple_of(j * ATT_TK, ATT_TK)
        return block(ks_ref[pl.ds(st, ATT_TK), :], vs_ref[pl.ds(st, ATT_TK), :], carry)

    init = (jnp.full((rows, 1), -jnp.inf, F32), jnp.zeros((rows, 1), F32), jnp.zeros((rows, HEAD_DIM), F32))
    carry = lax.fori_loop(0, s_len // ATT_TK, step, init)
    m, l, acc = block(ks_ref[s_len:s_len + n_ctx, :], vs_ref[s_len:s_len + n_ctx, :], carry)
    o = acc / l
    tq = rows // ATT_REP
    for r in range(ATT_REP):
        o_ref[0, :, r * HEAD_DIM:(r + 1) * HEAD_DIM] = o[r * tq:(r + 1) * tq]


def gqa_attention(p_l, p_c, q_norm, k_norm, cos_i, sin_i):
    b, s, _ = p_l.shape
    n_ctx = p_c.shape[1]
    gw = ATT_REP * HEAD_DIM
    vec = pl.BlockSpec((1, HEAD_DIM), lambda bi, g, i: (0, 0))
    full_tab = pl.BlockSpec((s, HEAD_DIM), lambda bi, g, i: (0, 0))
    q_tab = pl.BlockSpec((ATT_TQ, HEAD_DIM), lambda bi, g, i: (i, 0))
    return pl.pallas_call(
        _attn_kernel,
        out_shape=jax.ShapeDtypeStruct((b, s, GROUP_W), F32),
        grid=(b, ATT_KV_HEADS, s // ATT_TQ),
        in_specs=[pl.BlockSpec((1, ATT_TQ, gw), lambda bi, g, i: (bi, i, g)),
                  pl.BlockSpec((1, s, HEAD_DIM), lambda bi, g, i: (bi, 0, COL_AK + g)),
                  pl.BlockSpec((1, s, HEAD_DIM), lambda bi, g, i: (bi, 0, COL_AV + g)),
                  pl.BlockSpec((1, n_ctx, HEAD_DIM), lambda bi, g, i: (bi, 0, COL_AK + g)),
                  pl.BlockSpec((1, n_ctx, HEAD_DIM), lambda bi, g, i: (bi, 0, COL_AV + g)),
                  vec, vec, q_tab, q_tab, full_tab, full_tab],
        out_specs=pl.BlockSpec((1, ATT_TQ, gw), lambda bi, g, i: (bi, i, g)),
        scratch_shapes=[pltpu.VMEM((s + n_ctx, HEAD_DIM), BF16), pltpu.VMEM((s + n_ctx, HEAD_DIM), BF16)],
        compiler_params=_cparams(("parallel", "parallel", "arbitrary")),
        name="gqa_attention",
    )(p_l, p_l, p_l, p_c, p_c, q_norm.reshape(1, HEAD_DIM), k_norm.reshape(1, HEAD_DIM),
      cos_i, sin_i, cos_i, sin_i)


def _ctx_attn_kernel(p_ref, qn_ref, kn_ref, oa_ref, ob_ref):
    scale = HEAD_DIM ** -0.5

    def col(c):
        return p_ref[0, :, c * HEAD_DIM:(c + 1) * HEAD_DIM]

    def attend(q, k, v):
        s = lax.dot_general((q * scale).astype(BF16), k.astype(BF16), NT_DIMS, preferred_element_type=F32)
        p = jnp.exp(s - jnp.max(s, axis=-1, keepdims=True))
        o = jnp.dot(p.astype(BF16), v.astype(BF16), preferred_element_type=F32)
        return o / jnp.sum(p, axis=-1, keepdims=True)

    for h in range(ATT_HEADS):
        g = h // ATT_REP
        q = _head_rms(col(h), qn_ref[...])
        k = _head_rms(col(COL_AK + g), kn_ref[...])
        oa_ref[0, :, h * HEAD_DIM:(h + 1) * HEAD_DIM] = attend(q, k, col(COL_AV + g))
    for h in range(NA_HEADS):
        ob_ref[0, :, h * HEAD_DIM:(h + 1) * HEAD_DIM] = attend(col(COL_BQ + h), col(COL_BK + h), col(COL_BV + h))


def ctx_attention(p_c, q_norm, k_norm):
    b, n_ctx, n = p_c.shape
    vec = pl.BlockSpec((1, HEAD_DIM), lambda bi: (0, 0))
    out = pl.BlockSpec((1, n_ctx, GROUP_W), lambda bi: (bi, 0, 0))
    return pl.pallas_call(
        _ctx_attn_kernel,
        out_shape=(jax.ShapeDtypeStruct((b, n_ctx, GROUP_W), F32),) * 2,
        grid=(b,),
        in_specs=[pl.BlockSpec((1, n_ctx, n), lambda bi: (bi, 0, 0)), vec, vec],
        out_specs=(out, out),
        compiler_params=_cparams(("parallel",)),
        name="ctx_attention",
    )(p_c, q_norm.reshape(1, HEAD_DIM), k_norm.reshape(1, HEAD_DIM))


def _rms_norm(x, g):
    xf = x.astype(F32)
    y = xf * lax.rsqrt(jnp.mean(xf * xf, axis=-1, keepdims=True) + EPS)
    return (y * g.astype(F32)).astype(x.dtype)


def _heads(t, n):
    return t.reshape(t.shape[:-1] + (n, t.shape[-1] // n))


def _spatial_gating_unit(u, v, norm_g, w_s, b_s):
    bsz, length, _ = v.shape
    n = length // SGU_CHUNK
    vn = _rms_norm(v, norm_g).reshape(bsz, n, SGU_CHUNK, SGU_GROUPS, SGU_GROUP_W)
    mixed = jnp.einsum('gij,bnjgc->bnigc', w_s, vn) + b_s.T[:, :, None]
    return u * mixed.reshape(bsz, length, GROUP_W)


def _depthwise_conv(t, w, b):
    y = lax.conv_general_dilated(t, w[:, None, :].astype(t.dtype), window_strides=(1,),
                                 padding=[(CONV_K // 2, CONV_K // 2)],
                                 dimension_numbers=('NWC', 'WIO', 'NWC'),
                                 feature_group_count=t.shape[-1])
    return y + b


def _ssm_inputs(dx, db, dc, ddt, conv_w, conv_b, dt_bias):
    parts = [dx, db] if dc is None else [dx, db, dc]
    xbc = jnp.concatenate(parts, axis=-1)
    n_ch = xbc.shape[-1]
    xbc = jax.nn.silu(_depthwise_conv(xbc, conv_w[:, :n_ch], conv_b[:n_ch]))
    bsz, length = dx.shape[:2]
    bc_w = SSM_GROUPS * SSM_STATE
    xs = xbc[..., :GROUP_W].reshape(bsz, length, SSM_GROUPS, SSM_HPG, SSM_HEAD_DIM)
    bm = xbc[..., GROUP_W:GROUP_W + bc_w].reshape(bsz, length, SSM_GROUPS, SSM_STATE)
    cm = None if dc is None else xbc[..., GROUP_W + bc_w:].reshape(bsz, length, SSM_GROUPS, SSM_STATE)
    dt = jax.nn.softplus(ddt.astype(F32).reshape(bsz, length, 2, SSM_GROUPS, SSM_HPG)
                         + dt_bias.astype(F32).reshape(2, SSM_GROUPS, SSM_HPG))
    return xs, bm, cm, dt


def _ssd_chunked(x, dt, a, bm, cm, h0):
    bsz, length, g, r, p = x.shape
    nc = length // SSM_CHUNK

    def chunks(t):
        return t.astype(F32).reshape((bsz, nc, SSM_CHUNK) + t.shape[2:])

    xc, dtc, bc, cc = chunks(x), chunks(dt), chunks(bm), chunks(cm)
    cum = jnp.cumsum(dtc * a, axis=2)
    xdt = xc * dtc[..., None]
    tri = jnp.tril(jnp.ones((SSM_CHUNK, SSM_CHUNK), dtype=bool))
    seg = jnp.where(tri[None, None, :, :, None, None], cum[:, :, :, None] - cum[:, :, None, :], -jnp.inf)
    scores = jnp.einsum('bcign,bcjgn->bcijg', cc, bc)
    y_diag = jnp.einsum('bcijg,bcijgr,bcjgrp->bcigrp', scores, jnp.exp(seg), xdt)
    decay_end = jnp.exp(cum[:, :, -1:] - cum)
    states = jnp.einsum('bcjgn,bcjgr,bcjgrp->bcgrpn', bc, decay_end, xdt)
    chunk_decay = jnp.exp(cum[:, :, -1])

    def carry_step(h, inp):
        s, dcy = inp
        return h * dcy[..., None, None] + s, h

    h_last, h_in = lax.scan(carry_step, h0.astype(F32),
                            (jnp.moveaxis(states, 1, 0), jnp.moveaxis(chunk_decay, 1, 0)))
    h_in = jnp.moveaxis(h_in, 0, 1)
    y_off = jnp.einsum('bcign,bcgrpn,bcigr->bcigrp', cc, h_in, jnp.exp(cum))
    return (y_diag + y_off).reshape(bsz, length, g, r, p), h_last


def _ssd_final_state(x, dt, a, bm):
    dt = dt.astype(F32)
    cum = jnp.cumsum(dt * a, axis=1)
    w = jnp.exp(cum[:, -1:] - cum) * dt
    return jnp.einsum('blgn,blgr,blgrp->bgrpn', bm.astype(F32), w, x.astype(F32))


def _orient(t, d):
    return jnp.flip(t, axis=1) if d == 1 else t


def _bidirectional_ssd(lat, ctx, a_log, d_skip, ctx_out):
    x_l, b_l, c_l, dt_l = lat
    x_c, b_c, c_c, dt_c = ctx
    h0 = jnp.zeros((x_l.shape[0], SSM_GROUPS, SSM_HPG, SSM_HEAD_DIM, SSM_STATE), F32)
    y_l, y_c = 0.0, 0.0
    for d in range(2):
        a = -jnp.exp(a_log[d].astype(F32)).reshape(SSM_GROUPS, SSM_HPG)
        skip = d_skip[d].astype(F32).reshape(SSM_GROUPS, SSM_HPG, 1)
        if ctx_out:
            yc, h_c = _ssd_chunked(_orient(x_c, d), _orient(dt_c[:, :, d], d), a, _orient(b_c, d), _orient(c_c, d), h0)
            y_c = y_c + _orient(yc, d) + skip * x_c.astype(F32)
        else:
            h_c = _ssd_final_state(_orient(x_c, d), _orient(dt_c[:, :, d], d), a, _orient(b_c, d))
        yl, _ = _ssd_chunked(_orient(x_l, d), _orient(dt_l[:, :, d], d), a, _orient(b_l, d), _orient(c_l, d), h_c)
        y_l = y_l + _orient(yl, d) + skip * x_l.astype(F32)
    return y_l, (y_c if ctx_out else None)


MOE_MB = 1024


def _moe(x, h, gate, router, w13, w2):
    bsz, l, d = h.shape
    t = h.reshape(-1, d)
    n_tok = t.shape[0]
    n_assign = n_tok * TOP_K
    n_blocks = -(-n_assign // MOE_MB) + N_EXPERTS
    n_slots = n_blocks * MOE_MB
    logits = jnp.dot(t, router, precision=lax.Precision.HIGHEST).astype(F32)
    top_v, top_e = lax.top_k(logits, TOP_K)
    top_w = jax.nn.softmax(top_v, axis=-1).reshape(-1)
    flat_e = top_e.reshape(-1)
    order = jnp.argsort(flat_e * n_assign + jnp.arange(n_assign))
    sorted_e = flat_e[order]
    counts = jnp.zeros((N_EXPERTS,), jnp.int32).at[flat_e].add(1)
    padded = (counts + MOE_MB - 1) // MOE_MB * MOE_MB
    seg_start = jnp.cumsum(counts) - counts
    pad_end = jnp.cumsum(padded)
    dest = pad_end[sorted_e] - padded[sorted_e] + jnp.arange(n_assign) - seg_start[sorted_e]
    slot_tok = jnp.zeros((n_slots,), jnp.int32).at[dest].set((order // TOP_K).astype(jnp.int32))
    slot_w = jnp.zeros((n_slots,), F32).at[dest].set(top_w[order])
    block_e = jnp.minimum(jnp.searchsorted(pad_end, jnp.arange(n_blocks) * MOE_MB, side='right'),
                          N_EXPERTS - 1).astype(jnp.int32)
    xs = t[slot_tok].astype(BF16)
    ys = expert_ffn(block_e, xs, w13, w2, MOE_MB)
    out = jnp.zeros_like(t).at[slot_tok].add(ys * slot_w[:, None])
    return x + gate * out.reshape(bsz, l, d)


def _mixers(p_l, dt_l, p_c, dt_c, tabs, ctx_out, q_norm, k_norm, rpb, sgu_norm, sgu_w, sgu_b, conv_w, conv_b,
            a_log, dt_bias, d_skip):
    bsz, s_len, _ = p_l.shape
    cuts = [int(v) for v in np.cumsum(Q_SIDE + K_SIDE[:-1])[:-1]]
    aq_l, bq_l, cu_l, cv_l, dz_l, dc_l, ak_l, av_l, bk_l, bv_l, dx_l, db_l = jnp.split(p_l, cuts, axis=-1)
    aq_c, bq_c, cu_c, cv_c, dz_c, dc_c, ak_c, av_c, bk_c, bv_c, dx_c, db_c = jnp.split(p_c, cuts, axis=-1)
    ddt_l = dt_l[..., :2 * SSM_HEADS]
    ddt_c = dt_c[..., :2 * SSM_HEADS]
    if not ctx_out:
        dc_c = None

    o_a_l = gqa_attention(p_l, p_c, q_norm, k_norm, *tabs)
    o_b_l = neighbourhood_attention(p_l, p_c, na_bias_table(rpb))
    o_c_l = _spatial_gating_unit(jax.nn.gelu(cu_l), jax.nn.gelu(cv_l), sgu_norm, sgu_w, sgu_b)
    y_l, y_c = _bidirectional_ssd(_ssm_inputs(dx_l, db_l, dc_l, ddt_l, conv_w, conv_b, dt_bias),
                                  _ssm_inputs(dx_c, db_c, dc_c, ddt_c, conv_w, conv_b, dt_bias),
                                  a_log, d_skip, ctx_out)
    o_d_l = y_l.reshape(bsz, s_len, GROUP_W) * jax.nn.silu(dz_l)
    lat = (o_a_l, o_b_l, o_c_l, o_d_l)
    if not ctx_out:
        return lat, None
    n_ctx = p_c.shape[1]
    o_a_c, o_b_c = ctx_attention(p_c, q_norm, k_norm)
    o_c_c = _spatial_gating_unit(jax.nn.gelu(cu_c), jax.nn.gelu(cv_c), sgu_norm, sgu_w, sgu_b)
    o_d_c = y_c.reshape(bsz, n_ctx, GROUP_W) * jax.nn.silu(dz_c)
    return lat, (o_a_c, o_b_c, o_c_c, o_d_c)


def kernel(x, c, ctx, c_ctx, w_mod, b_mod, norm_mix, norm_ffn, w_in, q_norm, k_norm, rpb, sgu_norm, sgu_w, sgu_b,
           conv_w, conv_b, a_log, dt_bias, d_skip, out_norm, w_out, ffn_w13, ffn_w2, router, moe_w13, moe_w2,
           final_norm):
    bsz, s_len, d = x.shape
    n_ctx = ctx.shape[1]
    tabs = rope_tables(s_len)
    cond8 = jnp.concatenate([c, c_ctx[None, :], jnp.zeros((8 - bsz - 1, d), F32)], axis=0)
    xc = ctx
    for i in range(DEPTH):
        ctx_out = i < DEPTH - 1
        m = modulation_rows(cond8, w_mod[i], b_mod[i])
        ml = [m[:bsz, k * d:(k + 1) * d][:, None, :] for k in range(6)]
        mc = [jnp.broadcast_to(m[bsz, k * d:(k + 1) * d][None, None, :], (bsz, 1, d)) for k in range(6)]
        w_dt = jnp.pad(w_in[i][:, IN_MAIN:], ((0, 0), (0, DT_PAD - 2 * SSM_HEADS)))
        p_l, dt_l = in_proj(x, norm_mix[i], ml[1], ml[0], w_in[i], w_dt, 1024)
        p_c, dt_c = in_proj(xc, norm_mix[i], mc[1], mc[0], w_in[i], w_dt, n_ctx)
        lat, cx = _mixers(p_l, dt_l, p_c, dt_c, tabs, ctx_out, q_norm[i], k_norm[i], rpb[i], sgu_norm[i], sgu_w[i],
                          sgu_b[i], conv_w[i], conv_b[i], a_log[i], dt_bias[i], d_skip[i])
        x = out_proj(*lat, out_norm[i], w_out[i], x, ml[2], 1024)
        if i % 2 == 0:
            x = dense_ffn(x, norm_ffn[i], ml[4], ml[3], ml[5], ffn_w13[i // 2], ffn_w2[i // 2], 512)
        else:
            h = _rms_norm(x, norm_ffn[i]) * (1 + ml[4]) + ml[3]
            x = _moe(x, h, ml[5], router[i // 2], moe_w13[i // 2], moe_w2[i // 2])
        if ctx_out:
            xc = out_proj(*cx, out_norm[i], w_out[i], xc, mc[2], n_ctx)
            if i % 2 == 0:
                xc = dense_ffn(xc, norm_ffn[i], mc[4], mc[3], mc[5], ffn_w13[i // 2], ffn_w2[i // 2], n_ctx)
            else:
                h = _rms_norm(xc, norm_ffn[i]) * (1 + mc[4]) + mc[3]
                xc = _moe(xc, h, mc[5], router[i // 2], moe_w13[i // 2], moe_w2[i // 2])
    return _rms_norm(x, final_norm)
```
